```python
import jax, jax.numpy as jnp
from jax import lax
import numpy as np

D_MODEL = 1024
BATCH = 4
SEQ = 4096
DEPTH = 2
DEC_BATCH = 32
DEC_SEQ = 4
PAST_LEN = 8192
PAGE_SIZE = 128

HEAD_DIM = 128
SB_HEADS = 12
BRANCH_WIDTH = 1536
MEM_HEADS = 4
MEM_WIDTH = 512
MIX_WIDTH = 2048
MEM_LEN = 256
POOL_WINDOWS = (2, 4, 8, 16)
POOL_GROUPS = 4
POOL_GROUP_WIDTH = 384
POOL_STATE_LEN = 15
POOL_IN_WIDTH = 4096
SB_IN_WIDTH = 7168
SB_BIAS_INIT = -7.0
Q_BLOCK = 128
RMS_EPS = 1e-6
N_POOL_LAYERS = (DEPTH + 1) // 2
N_SB_LAYERS = DEPTH // 2

kernel_name = "hybrid_pool_stickbreak_memory_decode_step"


def rms_norm(x, g):
    xf = x.astype(jnp.float32)
    y = xf * lax.rsqrt(jnp.mean(xf * xf, axis=-1, keepdims=True) + RMS_EPS)
    return (y * g.astype(jnp.float32)).astype(x.dtype)


def pool_mix(u_ext, pos, w_grp, scale):
    B, L, _ = u_ext.shape
    T = L - POOL_STATE_LEN
    uf = u_ext.astype(jnp.float32).reshape(B, L, POOL_GROUPS, POOL_GROUP_WIDTH)
    csum = jnp.concatenate([jnp.zeros_like(uf[:, :1]), jnp.cumsum(uf, axis=1)], axis=1)
    end = csum[:, POOL_STATE_LEN + 1:]
    means = []
    for g, w in enumerate(POOL_WINDOWS):
        start = csum[:, POOL_STATE_LEN + 1 - w:POOL_STATE_LEN + 1 - w + T, g]
        cnt = jnp.minimum(pos + 1, w).astype(jnp.float32)[None, :, None]
        means.append((end[:, :, g] - start) / cnt)
    pooled = jnp.stack(means, axis=2) - uf[:, POOL_STATE_LEN:]
    mixed = jnp.einsum('btgc,gcd->btgd', pooled, w_grp.astype(jnp.float32))
    return (mixed.reshape(B, T, BRANCH_WIDTH) * scale.astype(jnp.float32)).astype(u_ext.dtype)


def sb_block(q_blk, qpos_blk, k, v, kpos, bias):
    z = jnp.einsum('bqhd,bkhd->bhqk', q_blk.astype(jnp.float32), k.astype(jnp.float32)) * (HEAD_DIM ** -0.5)
    z = z + bias.astype(jnp.float32)[None, :, None, None]
    mask = kpos[None, :] < qpos_blk[:, None]
    log_beta = jax.nn.log_sigmoid(z)
    log_rem = jnp.where(mask, log_beta - z, 0.0)
    later = lax.cumsum(log_rem, axis=3, reverse=True) - log_rem
    a = jnp.where(mask, jnp.exp(log_beta + later), 0.0)
    return jnp.einsum('bhqk,bkhd->bqhd', a, v.astype(jnp.float32))


def stick_breaking(q, k, v, q_pos, k_pos, bias):
    B, Tq, H, Dh = q.shape
    blk = Q_BLOCK if Tq % Q_BLOCK == 0 else Tq
    nb = Tq // blk
    qb = q.reshape(B, nb, blk, H, Dh).transpose(1, 0, 2, 3, 4)
    pb = q_pos.reshape(nb, blk)
    out = lax.map(lambda a: sb_block(a[0], a[1], k, v, k_pos, bias), (qb, pb))
    return out.transpose(1, 0, 2, 3, 4).reshape(B, Tq, H * Dh).astype(q.dtype)


def mem_kv(mem, g_mem, w_mem_kv, g_k):
    B, M, _ = mem.shape
    kv = rms_norm(mem, g_mem) @ w_mem_kv
    k = rms_norm(kv[..., :MEM_WIDTH].reshape(B, M, MEM_HEADS, HEAD_DIM), g_k)
    v = kv[..., MEM_WIDTH:].reshape(B, M, MEM_HEADS, HEAD_DIM)
    return k, v


def merge_out(branch, gate, mq, mgate, mk, mv, g_q, w_out):
    B, T, _ = mq.shape
    q = rms_norm(mq.reshape(B, T, MEM_HEADS, HEAD_DIM), g_q)
    s = jnp.einsum('bqhd,bmhd->bhqm', q.astype(jnp.float32), mk.astype(jnp.float32)) * (HEAD_DIM ** -0.5)
    p = jax.nn.softmax(s, axis=-1)
    mo = jnp.einsum('bhqm,bmhd->bqhd', p, mv.astype(jnp.float32)).reshape(B, T, MEM_WIDTH).astype(mq.dtype)
    y = jnp.concatenate([branch * jax.nn.silu(gate), mo * jax.nn.silu(mgate)], axis=-1)
    return y @ w_out


def setup_inputs(seed: int = 0) -> dict:
    key = jax.random.key(seed)
    ks = jax.random.split(key, 21)
    f32 = jnp.float32
    n_pages = PAST_LEN // PAGE_SIZE
    n_used = DEC_BATCH * n_pages
    n_phys = n_used + n_used // 4

    def nrm(k, shape, scale=1.0):
        return jax.random.normal(k, shape, f32) * scale

    page_table = jax.random.permutation(ks[7], n_phys)[:n_used].reshape(DEC_BATCH, n_pages).astype(jnp.int32)
    return {
        "x_prompt": nrm(ks[0], (BATCH, SEQ, D_MODEL)),
        "x_sample": nrm(ks[1], (DEC_BATCH, DEC_SEQ, D_MODEL)),
        "cache_mem_k": nrm(ks[2], (DEPTH, DEC_BATCH, MEM_LEN, MEM_HEADS, HEAD_DIM)),
        "cache_mem_v": nrm(ks[3], (DEPTH, DEC_BATCH, MEM_LEN, MEM_HEADS, HEAD_DIM)),
        "state_pool": nrm(ks[4], (N_POOL_LAYERS, DEC_BATCH, POOL_STATE_LEN, BRANCH_WIDTH)),
        "cache_sb_k": nrm(ks[5], (N_SB_LAYERS, n_phys, PAGE_SIZE, SB_HEADS, HEAD_DIM)),
        "cache_sb_v": nrm(ks[6], (N_SB_LAYERS, n_phys, PAGE_SIZE, SB_HEADS, HEAD_DIM)),
        "page_table": page_table,
        "mem_prompt": nrm(ks[8], (BATCH, MEM_LEN, D_MODEL)),
        "norm_g": 1.0 + nrm(ks[9], (DEPTH, D_MODEL), 0.05),
        "w_in_pool": nrm(ks[10], (N_POOL_LAYERS, D_MODEL, POOL_IN_WIDTH), D_MODEL ** -0.5),
        "w_pool_grp": nrm(ks[11], (N_POOL_LAYERS, POOL_GROUPS, POOL_GROUP_WIDTH, POOL_GROUP_WIDTH), POOL_GROUP_WIDTH ** -0.5),
        "pool_scale": 1.0 + nrm(ks[12], (N_POOL_LAYERS, BRANCH_WIDTH), 0.05),
        "w_in_sb": nrm(ks[13], (N_SB_LAYERS, D_MODEL, SB_IN_WIDTH), D_MODEL ** -0.5),
        "sb_bias": SB_BIAS_INIT + nrm(ks[19], (N_SB_LAYERS, SB_HEADS), 0.3),
        "mem_norm_g": 1.0 + nrm(ks[14], (DEPTH, D_MODEL), 0.05),
        "w_mem_kv": nrm(ks[15], (DEPTH, D_MODEL, 2 * MEM_WIDTH), D_MODEL ** -0.5),
        "mem_q_g": 1.0 + nrm(ks[16], (DEPTH, HEAD_DIM), 0.05),
        "mem_k_g": 1.0 + nrm(ks[17], (DEPTH, HEAD_DIM), 0.05),
        "w_out": nrm(ks[18], (DEPTH, MIX_WIDTH, D_MODEL), MIX_WIDTH ** -0.5),
    }


def reference(x_prompt, x_sample, cache_mem_k, cache_mem_v, state_pool, cache_sb_k, cache_sb_v, page_table,
              mem_prompt, norm_g, w_in_pool, w_pool_grp, pool_scale, w_in_sb, sb_bias, mem_norm_g, w_mem_kv,
              mem_q_g, mem_k_g, w_out):
    B, T = x_prompt.shape[0], x_prompt.shape[1]
    Bd, Td = x_sample.shape[0], x_sample.shape[1]
    n_pages = page_table.shape[1]
    past = n_pages * cache_sb_k.shape[2]
    pos_p = jnp.arange(T, dtype=jnp.int32)
    pos_s = past + jnp.arange(Td, dtype=jnp.int32)
    kpos_s = jnp.arange(past + Td, dtype=jnp.int32)
    bw, mw = BRANCH_WIDTH, MEM_WIDTH

    xp, xs = x_prompt, x_sample
    mk_out, mv_out = [], []
    pool_p_out, pool_s_out = [], []
    sbk_p_out, sbv_p_out, sbk_s_out, sbv_s_out = [], [], [], []
    for i in range(DEPTH):
        j = i // 2
        hp = rms_norm(xp, norm_g[i])
        hs = rms_norm(xs, norm_g[i])
        mk_p, mv_p = mem_kv(mem_prompt, mem_norm_g[i], w_mem_kv[i], mem_k_g[i])
        mk_out.append(mk_p)
        mv_out.append(mv_p)
        if i % 2 == 0:
            zp = hp @ w_in_pool[j]
            zs = hs @ w_in_pool[j]
            u_p, gate_p, mq_p, mg_p = zp[..., :bw], zp[..., bw:2 * bw], zp[..., 2 * bw:2 * bw + mw], zp[..., 2 * bw + mw:]
            u_s, gate_s, mq_s, mg_s = zs[..., :bw], zs[..., bw:2 * bw], zs[..., 2 * bw:2 * bw + mw], zs[..., 2 * bw + mw:]
            u_p_ext = jnp.concatenate([jnp.zeros((B, POOL_STATE_LEN, bw), u_p.dtype), u_p], axis=1)
            u_s_ext = jnp.concatenate([state_pool[j].astype(u_s.dtype), u_s], axis=1)
            br_p = pool_mix(u_p_ext, pos_p, w_pool_grp[j], pool_scale[j])
            br_s = pool_mix(u_s_ext, pos_s, w_pool_grp[j], pool_scale[j])
            pool_p_out.append(u_p_ext[:, -POOL_STATE_LEN:])
            pool_s_out.append(u_s_ext[:, -POOL_STATE_LEN:])
        else:
            zp = hp @ w_in_sb[j]
            zs = hs @ w_in_sb[j]
            q_p = zp[..., :bw].reshape(B, T, SB_HEADS, HEAD_DIM)
            k_p = zp[..., bw:2 * bw].reshape(B, T, SB_HEADS, HEAD_DIM)
            v_p = zp[..., 2 * bw:3 * bw].reshape(B, T, SB_HEADS, HEAD_DIM)
            gate_p, mq_p, mg_p = zp[..., 3 * bw:4 * bw], zp[..., 4 * bw:4 * bw + mw], zp[..., 4 * bw + mw:]
            q_s = zs[..., :bw].reshape(Bd, Td, SB_HEADS, HEAD_DIM)
            k_s = zs[..., bw:2 * bw].reshape(Bd, Td, SB_HEADS, HEAD_DIM)
            v_s = zs[..., 2 * bw:3 * bw].reshape(Bd, Td, SB_HEADS, HEAD_DIM)
            gate_s, mq_s, mg_s = zs[..., 3 * bw:4 * bw], zs[..., 4 * bw:4 * bw + mw], zs[..., 4 * bw + mw:]
            br_p = stick_breaking(q_p, k_p, v_p, pos_p, pos_p, sb_bias[j])
            k_past = cache_sb_k[j][page_table].reshape(Bd, past, SB_HEADS, HEAD_DIM).astype(k_s.dtype)
            v_past = cache_sb_v[j][page_table].reshape(Bd, past, SB_HEADS, HEAD_DIM).astype(v_s.dtype)
            k_all = jnp.concatenate([k_past, k_s], axis=1)
            v_all = jnp.concatenate([v_past, v_s], axis=1)
            br_s = stick_breaking(q_s, k_all, v_all, pos_s, kpos_s, sb_bias[j])
            sbk_p_out.append(k_p)
            sbv_p_out.append(v_p)
            sbk_s_out.append(k_s)
            sbv_s_out.append(v_s)
        xp = xp + merge_out(br_p, gate_p, mq_p, mg_p, mk_p, mv_p, mem_q_g[i], w_out[i])
        xs = xs + merge_out(br_s, gate_s, mq_s, mg_s, cache_mem_k[i], cache_mem_v[i], mem_q_g[i], w_out[i])

    new_mem_k_prompt = jnp.stack(mk_out)
    new_mem_v_prompt = jnp.stack(mv_out)
    new_pool_state_prompt = jnp.stack(pool_p_out)
    new_pool_state_sample = jnp.stack(pool_s_out)
    new_sb_k_prompt = jnp.stack(sbk_p_out)
    new_sb_v_prompt = jnp.stack(sbv_p_out)
    new_sb_k_sample = jnp.stack(sbk_s_out)
    new_sb_v_sample = jnp.stack(sbv_s_out)
    return (xp, xs, new_mem_k_prompt, new_mem_v_prompt, new_pool_state_prompt, new_pool_state_sample,
            new_sb_k_prompt, new_sb_v_prompt, new_sb_k_sample, new_sb_v_sample)
```

```python
import functools

import jax
import jax.numpy as jnp
from jax import lax
from jax.experimental import pallas as pl
from jax.experimental.pallas import tpu as pltpu

HEAD_DIM = 128
SB_HEADS = 12
BRANCH_WIDTH = SB_HEADS * HEAD_DIM
MEM_HEADS = 4
MEM_WIDTH = MEM_HEADS * HEAD_DIM
POOL_WINDOWS = (2, 4, 8, 16)
POOL_GROUP_WIDTH = BRANCH_WIDTH // len(POOL_WINDOWS)
POOL_STATE_LEN = POOL_WINDOWS[-1] - 1
POOL_HALO = POOL_WINDOWS[-1]
RMS_EPS = 1e-6
ATTN_SCALE = HEAD_DIM ** -0.5
DEC_ROWS = 8
VMEM_LIMIT = 56 * 1024 * 1024

F32 = jnp.float32
BF16 = jnp.bfloat16


def _params(*sem):
    return pltpu.CompilerParams(dimension_semantics=sem, vmem_limit_bytes=VMEM_LIMIT)


def _const_spec(shape):
    return pl.BlockSpec(shape, lambda *_: (0,) * len(shape), pipeline_mode=pl.Buffered(1))


def _rms(x, g):
    return x * lax.rsqrt(jnp.mean(x * x, axis=-1, keepdims=True) + RMS_EPS) * g


def _silu(x):
    return x * (1.0 / (1.0 + jnp.exp(-x)))


def _softplus(z):
    return jnp.maximum(z, 0.0) + jnp.log(1.0 + jnp.exp(-jnp.abs(z)))


def _dot(a, b):
    return jnp.dot(a, b, preferred_element_type=F32)


def _dot_nt(a, b):
    return lax.dot_general(a, b, (((1,), (1,)), ((), ())), preferred_element_type=F32)


def _mem_attn(mq, mk, mv, g_q):
    outs = []
    for h in range(MEM_HEADS):
        sl = slice(h * HEAD_DIM, (h + 1) * HEAD_DIM)
        q = (_rms(mq[:, sl], g_q) * ATTN_SCALE).astype(BF16)
        s = _dot_nt(q, mk[:, sl])
        p = jnp.exp(s - jnp.max(s, axis=-1, keepdims=True))
        l = jnp.sum(p, axis=-1, keepdims=True)
        outs.append(_dot(p.astype(BF16), mv[:, sl]) / l)
    return jnp.concatenate(outs, axis=-1)


def _merge(x, br, gate, mq, mg, mk, mv, g_q, wout_ref):
    yb = (br * _silu(gate)).astype(BF16)
    ym = (_mem_attn(mq, mk, mv, g_q) * _silu(mg)).astype(BF16)
    return x + _dot(yb, wout_ref[:BRANCH_WIDTH, :]) + _dot(ym, wout_ref[BRANCH_WIDTH:, :])


def _pool_group_mix(pooled, wgrp_ref, scale):
    mixed = [_dot(pooled[g].astype(BF16), wgrp_ref[g]) for g in range(len(POOL_WINDOWS))]
    return jnp.concatenate(mixed, axis=-1) * scale


def _mem_kv_kernel(mem_ref, gm_ref, w_ref, gk_ref, k_ref, v_ref):
    h = _rms(mem_ref[...], gm_ref[...]).astype(BF16)
    kv = _dot(h, w_ref[...])
    gk = gk_ref[...]
    for hd in range(MEM_HEADS):
        sl = slice(hd * HEAD_DIM, (hd + 1) * HEAD_DIM)
        k_ref[:, sl] = _rms(kv[:, sl], gk)
    v_ref[...] = kv[:, MEM_WIDTH:]


def _mem_kv(mem, g_mem, w_mem_kv, g_k):
    depth = g_mem.shape[0]
    b, m, d = mem.shape
    out = jax.ShapeDtypeStruct((depth, b, m, MEM_WIDTH), F32)
    return pl.pallas_call(
        _mem_kv_kernel,
        out_shape=(out, out),
        grid=(depth, b),
        in_specs=[
            pl.BlockSpec((None, m, d), lambda i, j: (j, 0, 0)),
            pl.BlockSpec((None, 1, d), lambda i, j: (i, 0, 0)),
            pl.BlockSpec((None, d, 2 * MEM_WIDTH), lambda i, j: (i, 0, 0)),
            pl.BlockSpec((None, 1, HEAD_DIM), lambda i, j: (i, 0, 0)),
        ],
        out_specs=(pl.BlockSpec((None, None, m, MEM_WIDTH), lambda i, j: (i, j, 0, 0)),
                   pl.BlockSpec((None, None, m, MEM_WIDTH), lambda i, j: (i, j, 0, 0))),
        compiler_params=_params("arbitrary", "arbitrary"),
        name="mem_kv",
    )(mem, g_mem.reshape(depth, 1, d), w_mem_kv, g_k.reshape(depth, 1, HEAD_DIM))


def _pool_prompt_kernel(x_ref, g_ref, win_ref, wgrp_ref, scale_ref, mk_ref, mv_ref, gq_ref, wout_ref,
                        y_ref, state_ref, ext_ref, *, tb):
    t = pl.program_id(1)

    @pl.when(t == 0)
    def _():
        ext_ref[0:POOL_HALO, :] = jnp.zeros((POOL_HALO, BRANCH_WIDTH), F32)

    x = x_ref[...]
    z = _dot(_rms(x, g_ref[...]).astype(BF16), win_ref[...])
    u = z[:, :BRANCH_WIDTH]
    ext_ref[POOL_HALO:POOL_HALO + tb, :] = u

    pos = t * tb + lax.broadcasted_iota(jnp.int32, (tb, 1), 0)
    pooled = []
    for g, w in enumerate(POOL_WINDOWS):
        cols = slice(g * POOL_GROUP_WIDTH, (g + 1) * POOL_GROUP_WIDTH)
        wsum = u[:, cols]
        for s in range(1, w):
            wsum = wsum + ext_ref[POOL_HALO - s:POOL_HALO - s + tb, cols]
        cnt = jnp.minimum(pos + 1, w).astype(F32)
        pooled.append(wsum / cnt - u[:, cols])
    br = _pool_group_mix(pooled, wgrp_ref, scale_ref[...])

    b2 = 2 * BRANCH_WIDTH
    y_ref[...] = _merge(x, br, z[:, BRANCH_WIDTH:b2], z[:, b2:b2 + MEM_WIDTH], z[:, b2 + MEM_WIDTH:],
                        mk_ref[...].astype(BF16), mv_ref[...].astype(BF16), gq_ref[...], wout_ref)

    @pl.when(t == pl.num_programs(1) - 1)
    def _():
        state_ref[...] = ext_ref[tb + 1:tb + POOL_HALO, :]

    ext_ref[0:POOL_HALO, :] = ext_ref[tb:tb + POOL_HALO, :]


def _pool_layer_prompt(x, g, w_in, w_grp, scale, mk, mv, g_q, w_out, *, tb):
    b, t, d = x.shape
    m = mk.shape[1]
    n_in = w_in.shape[1]
    return pl.pallas_call(
        functools.partial(_pool_prompt_kernel, tb=tb),
        out_shape=(jax.ShapeDtypeStruct((b, t, d), F32),
                   jax.ShapeDtypeStruct((b, POOL_STATE_LEN, BRANCH_WIDTH), F32)),
        grid=(b, t // tb),
        in_specs=[
            pl.BlockSpec((None, tb, d), lambda i, j: (i, j, 0)),
            _const_spec((1, d)),
            _const_spec((d, n_in)),
            _const_spec(w_grp.shape),
            _const_spec((1, BRANCH_WIDTH)),
            pl.BlockSpec((None, m, MEM_WIDTH), lambda i, j: (i, 0, 0)),
            pl.BlockSpec((None, m, MEM_WIDTH), lambda i, j: (i, 0, 0)),
            _const_spec((1, HEAD_DIM)),
            _const_spec(w_out.shape),
        ],
        out_specs=(pl.BlockSpec((None, tb, d), lambda i, j: (i, j, 0)),
                   pl.BlockSpec((None, POOL_STATE_LEN, BRANCH_WIDTH), lambda i, j: (i, 0, 0))),
        scratch_shapes=[pltpu.VMEM((POOL_HALO + tb, BRANCH_WIDTH), F32)],
        compiler_params=_params("arbitrary", "arbitrary"),
        name="pool_layer_prompt",
    )(x, g.reshape(1, d), w_in, w_grp, scale.reshape(1, BRANCH_WIDTH), mk, mv, g_q.reshape(1, HEAD_DIM), w_out)


def _pool_decode_kernel(x_ref, g_ref, win_ref, wgrp_ref, scale_ref, state_ref,
                        br_ref, u_ref, tail_ref, pool_ref, *, n_tok, past):
    bd = state_ref.shape[1]
    z = _dot(_rms(x_ref[...], g_ref[...]).astype(BF16), win_ref[...])
    u_ref[...] = z[:, :BRANCH_WIDTH]
    tail_ref[...] = z[:, BRANCH_WIDTH:]

    def ext_row(j, cols):
        if j < POOL_STATE_LEN:
            return state_ref[j, :, cols]
        return u_ref[(j - POOL_STATE_LEN) * bd:(j - POOL_STATE_LEN + 1) * bd, cols]

    pool_ref[...] = jnp.zeros(pool_ref.shape, F32)
    for tok in range(n_tok):
        for g, w in enumerate(POOL_WINDOWS):
            cols = slice(g * POOL_GROUP_WIDTH, (g + 1) * POOL_GROUP_WIDTH)
            last = POOL_STATE_LEN + tok
            wsum = ext_row(last, cols)
            for s in range(1, w):
                wsum = wsum + ext_row(last - s, cols)
            cnt = float(min(past + tok + 1, w))
            pool_ref[tok * bd:(tok + 1) * bd, cols] = wsum / cnt - ext_row(last, cols)

    pooled = [pool_ref[:, g * POOL_GROUP_WIDTH:(g + 1) * POOL_GROUP_WIDTH] for g in range(len(POOL_WINDOWS))]
    br_ref[...] = _pool_group_mix(pooled, wgrp_ref, scale_ref[...])


def _pool_layer_decode_proj(x2d, g, w_in, w_grp, scale, state_tm, *, n_tok, past):
    rows, d = x2d.shape
    bd = state_tm.shape[1]
    tail_w = w_in.shape[1] - BRANCH_WIDTH

    def swap(a, lead):
        return a.reshape(lead, rows // lead, a.shape[-1]).transpose(1, 0, 2).reshape(rows, a.shape[-1])

    br, u, tail = pl.pallas_call(
        functools.partial(_pool_decode_kernel, n_tok=n_tok, past=past),
        out_shape=(jax.ShapeDtypeStruct((rows, BRANCH_WIDTH), F32),
                   jax.ShapeDtypeStruct((rows, BRANCH_WIDTH), F32),
                   jax.ShapeDtypeStruct((rows, tail_w), F32)),
        scratch_shapes=[pltpu.VMEM((rows, BRANCH_WIDTH), F32)],
        compiler_params=pltpu.CompilerParams(vmem_limit_bytes=VMEM_LIMIT),
        name="pool_layer_decode_proj",
    )(swap(x2d, bd), g.reshape(1, d), w_in, w_grp, scale.reshape(1, BRANCH_WIDTH), state_tm)
    return swap(br, DEC_ROWS), u.reshape(DEC_ROWS, bd, BRANCH_WIDTH), swap(tail, DEC_ROWS)


def _mem_decode_kernel(tail_ref, mk_ref, mv_ref, gq_ref, mo_ref):
    mq = tail_ref[:, BRANCH_WIDTH:BRANCH_WIDTH + MEM_WIDTH]
    mo = _mem_attn(mq, mk_ref[...].astype(BF16), mv_ref[...].astype(BF16), gq_ref[...])
    mo_ref[...] = mo * _silu(tail_ref[:, BRANCH_WIDTH + MEM_WIDTH:])


def _mem_decode(tail, mk, mv, g_q):
    bd, r, tail_w = tail.shape
    m = mk.shape[1]
    return pl.pallas_call(
        _mem_decode_kernel,
        out_shape=jax.ShapeDtypeStruct((bd, r, MEM_WIDTH), F32),
        grid=(bd,),
        in_specs=[
            pl.BlockSpec((None, r, tail_w), lambda i: (i, 0, 0)),
            pl.BlockSpec((None, m, MEM_WIDTH), lambda i: (i, 0, 0)),
            pl.BlockSpec((None, m, MEM_WIDTH), lambda i: (i, 0, 0)),
            _const_spec((1, HEAD_DIM)),
        ],
        out_specs=pl.BlockSpec((None, r, MEM_WIDTH), lambda i: (i, 0, 0)),
        compiler_params=_params("arbitrary"),
        name="mem_decode",
    )(tail, mk, mv, g_q.reshape(1, HEAD_DIM))


def _out_decode_kernel(x_ref, br_ref, tail_ref, mog_ref, wout_ref, y_ref):
    yb = (br_ref[...] * _silu(tail_ref[:, :BRANCH_WIDTH])).astype(BF16)
    y_ref[...] = (x_ref[...] + _dot(yb, wout_ref[:BRANCH_WIDTH, :])
                  + _dot(mog_ref[...].astype(BF16), wout_ref[BRANCH_WIDTH:, :]))


def _out_decode(x2d, br, tail, mog, w_out):
    return pl.pallas_call(
        _out_decode_kernel,
        out_shape=jax.ShapeDtypeStruct(x2d.shape, F32),
        compiler_params=pltpu.CompilerParams(vmem_limit_bytes=VMEM_LIMIT),
        name="out_decode",
    )(x2d, br, tail, mog, w_out)


def _sb_proj_kernel(x_ref, g_ref, win_ref, q_ref, k_ref, v_ref, kb_ref, vb_ref, tail_ref):
    z = _dot(_rms(x_ref[...], g_ref[...]).astype(BF16), win_ref[...])
    bw = BRANCH_WIDTH
    for h in range(SB_HEADS):
        lo, hi = h * HEAD_DIM, (h + 1) * HEAD_DIM
        q_ref[h] = (z[:, lo:hi] * ATTN_SCALE).astype(q_ref.dtype)
        k = z[:, bw + lo:bw + hi]
        v = z[:, 2 * bw + lo:2 * bw + hi]
        k_ref[h] = k
        v_ref[h] = v
        kb_ref[h] = k.astype(BF16)
        vb_ref[h] = v.astype(BF16)
    tail_ref[...] = z[:, 3 * bw:]


def _sb_proj(x, g, w_in, *, tb):
    b, t, d = x.shape
    n_in = w_in.shape[1]
    tail_w = n_in - 3 * BRANCH_WIDTH
    head_shape = (b, SB_HEADS, t, HEAD_DIM)
    head_spec = pl.BlockSpec((None, SB_HEADS, tb, HEAD_DIM), lambda i, j: (i, 0, j, 0))
    return pl.pallas_call(
        _sb_proj_kernel,
        out_shape=(jax.ShapeDtypeStruct(head_shape, BF16),
                   jax.ShapeDtypeStruct(head_shape, F32),
                   jax.ShapeDtypeStruct(head_shape, F32),
                   jax.ShapeDtypeStruct(head_shape, BF16),
                   jax.ShapeDtypeStruct(head_shape, BF16),
                   jax.ShapeDtypeStruct((b, t, tail_w), F32)),
        grid=(b, t // tb),
        in_specs=[pl.BlockSpec((None, tb, d), lambda i, j: (i, j, 0)), _const_spec((1, d)), _const_spec((d, n_in))],
        out_specs=(head_spec, head_spec, head_spec, head_spec, head_spec,
                   pl.BlockSpec((None, tb, tail_w), lambda i, j: (i, j, 0))),
        compiler_params=_params("arbitrary", "arbitrary"),
        name="sb_proj",
    )(x, g.reshape(1, d), w_in)


def _sb_weights(z, tri, c, mask):
    sp = _softplus(z)
    if mask is not None:
        sp = jnp.where(mask, sp, 0.0)
    cum = _dot(sp.astype(BF16), tri)
    a = jnp.exp(z - cum - c)
    if mask is not None:
        a = jnp.where(mask, a, 0.0)
    return a, c + cum[:, 0:1]


def _tri(n):
    return (lax.broadcasted_iota(jnp.int32, (n, n), 0) >= lax.broadcasted_iota(jnp.int32, (n, n), 1)).astype(BF16)


def _sb_prompt_kernel(bias_ref, q_ref, k_ref, v_ref, o_ref, *, tq):
    qi = pl.program_id(2)
    q = q_ref[...]
    bias = bias_ref[...]
    tri = _tri(tq)
    causal = lax.broadcasted_iota(jnp.int32, (tq, tq), 1) < lax.broadcasted_iota(jnp.int32, (tq, tq), 0)

    def step(j, carry, mask):
        acc, c = carry
        start = pl.multiple_of(j * tq, tq)
        z = _dot_nt(q, k_ref[pl.ds(start, tq), :]) + bias
        a, c = _sb_weights(z, tri, c, mask)
        return acc + _dot(a.astype(BF16), v_ref[pl.ds(start, tq), :]), c

    carry = (jnp.zeros((tq, HEAD_DIM), F32), jnp.zeros((tq, 1), F32))
    carry = step(qi, carry, causal)
    carry = lax.fori_loop(0, qi, lambda i, cr: step(qi - 1 - i, cr, None), carry)
    o_ref[...] = carry[0]


def _sb_attn_prompt(q, k, v, bias, *, tq):
    b, _, t, _ = q.shape
    bias_rows = jnp.broadcast_to(bias.astype(F32)[:, None, None], (SB_HEADS, 1, tq))
    kv_spec = pl.BlockSpec((None, None, t, HEAD_DIM), lambda i, h, j: (i, h, 0, 0))
    return pl.pallas_call(
        functools.partial(_sb_prompt_kernel, tq=tq),
        out_shape=jax.ShapeDtypeStruct((b, t, BRANCH_WIDTH), F32),
        grid=(b, SB_HEADS, t // tq),
        in_specs=[
            pl.BlockSpec((None, 1, tq), lambda i, h, j: (h, 0, 0)),
            pl.BlockSpec((None, None, tq, HEAD_DIM), lambda i, h, j: (i, h, j, 0)),
            kv_spec, kv_spec,
        ],
        out_specs=pl.BlockSpec((None, tq, HEAD_DIM), lambda i, h, j: (i, j, h)),
        compiler_params=_params("arbitrary", "arbitrary", "arbitrary"),
        name="sb_attn_prompt",
    )(bias_rows, q, k, v)


def _sb_proj_decode_kernel(x_ref, g_ref, win_ref, q_ref, k_ref, v_ref, tail_ref):
    z = _dot(_rms(x_ref[...], g_ref[...]).astype(BF16), win_ref[...])
    bw = BRANCH_WIDTH
    for h in range(SB_HEADS):
        lo, hi = h * HEAD_DIM, (h + 1) * HEAD_DIM
        q_ref[h] = (z[:, lo:hi] * ATTN_SCALE).astype(q_ref.dtype)
        k_ref[h] = z[:, bw + lo:bw + hi]
        v_ref[h] = z[:, 2 * bw + lo:2 * bw + hi]
    tail_ref[...] = z[:, 3 * bw:]


def _sb_proj_decode(x2d, g, w_in):
    rows, d = x2d.shape
    head_shape = (SB_HEADS, rows, HEAD_DIM)
    return pl.pallas_call(
        _sb_proj_decode_kernel,
        out_shape=(jax.ShapeDtypeStruct(head_shape, F32),
                   jax.ShapeDtypeStruct(head_shape, F32),
                   jax.ShapeDtypeStruct(head_shape, F32),
                   jax.ShapeDtypeStruct((rows, w_in.shape[1] - 3 * BRANCH_WIDTH), F32)),
        compiler_params=pltpu.CompilerParams(vmem_limit_bytes=VMEM_LIMIT),
        name="sb_proj_decode",
    )(x2d, g.reshape(1, d), w_in)


def _sb_decode_block(q, k3, v3, bias, tri, head_of_row, carry, mask):
    acc, c = carry
    kb = k3.shape[1]
    s = _dot_nt(q, k3.reshape(SB_HEADS * kb, HEAD_DIM).astype(BF16))
    z = jnp.zeros((q.shape[0], kb), F32)
    for h in range(SB_HEADS):
        z = jnp.where(head_of_row == h, s[:, h * kb:(h + 1) * kb], z)
    a, c = _sb_weights(z + bias, tri, c, mask)
    vcat = jnp.concatenate([v3[h].astype(BF16) for h in range(SB_HEADS)], axis=1)
    return acc + _dot(a.astype(BF16), vcat), c


def _sb_decode_kernel(pt_ref, bias_ref, q_ref, kn_ref, vn_ref, *refs, pages_per_step):
    del pt_ref
    k_refs = refs[:pages_per_step]
    v_refs = refs[pages_per_step:2 * pages_per_step]
    o_ref, acc_ref, c_ref = refs[2 * pages_per_step:]
    s = pl.program_id(1)
    rows = SB_HEADS * DEC_ROWS
    page = k_refs[0].shape[1]
    bias = bias_ref[...]
    tri = _tri(page)
    q = q_ref[...].reshape(rows, HEAD_DIM).astype(BF16)
    head_of_row = lax.broadcasted_iota(jnp.int32, (rows, page), 0) // DEC_ROWS

    @pl.when(s == 0)
    def _():
        pad = jnp.zeros((SB_HEADS, page - DEC_ROWS, HEAD_DIM), F32)
        kn = jnp.concatenate([kn_ref[...], pad], axis=1)
        vn = jnp.concatenate([vn_ref[...], pad], axis=1)
        key = lax.broadcasted_iota(jnp.int32, (rows, page), 1)
        tok_of_row = lax.broadcasted_iota(jnp.int32, (rows, page), 0) % DEC_ROWS
        carry = (jnp.zeros((rows, BRANCH_WIDTH), F32), jnp.zeros((rows, 1), F32))
        acc, c = _sb_decode_block(q, kn, vn, bias, tri, head_of_row, carry, key < tok_of_row)
        acc_ref[...] = acc
        c_ref[...] = c

    for p in range(pages_per_step):
        carry = (acc_ref[...], c_ref[...])
        acc, c = _sb_decode_block(q, k_refs[p][...], v_refs[p][...], bias, tri, head_of_row, carry, None)
        acc_ref[...] = acc
        c_ref[...] = c

    @pl.when(s == pl.num_programs(1) - 1)
    def _():
        for h in range(SB_HEADS):
            cols = slice(h * HEAD_DIM, (h + 1) * HEAD_DIM)
            o_ref[:, cols] = acc_ref[h * DEC_ROWS:(h + 1) * DEC_ROWS, cols]


def _sb_attn_decode(q, k_new, v_new, cache_k, cache_v, page_table, bias, *, pages_per_step):
    bd, n_pages = page_table.shape
    page = cache_k.shape[2]
    rows = SB_HEADS * DEC_ROWS
    assert n_pages % pages_per_step == 0 and page >= DEC_ROWS
    n_steps = n_pages // pages_per_step
    bias_col = jnp.repeat(bias.astype(F32), DEC_ROWS).reshape(rows, 1)

    def page_spec(p):
        return pl.BlockSpec((None, SB_HEADS, page, HEAD_DIM),
                            lambda i, s, pt: (pt[i, n_pages - 1 - (s * pages_per_step + p)], 0, 0, 0))

    tok_spec = pl.BlockSpec((SB_HEADS, DEC_ROWS, HEAD_DIM), lambda i, s, pt: (0, i, 0))
    return pl.pallas_call(
        functools.partial(_sb_decode_kernel, pages_per_step=pages_per_step),
        out_shape=jax.ShapeDtypeStruct((bd, DEC_ROWS, BRANCH_WIDTH), F32),
        grid_spec=pltpu.PrefetchScalarGridSpec(
            num_scalar_prefetch=1,
            grid=(bd, n_steps),
            in_specs=[pl.BlockSpec((rows, 1), lambda i, s, pt: (0, 0)), tok_spec, tok_spec, tok_spec]
                     + [page_spec(p) for p in range(pages_per_step)]
                     + [page_spec(p) for p in range(pages_per_step)],
            out_specs=pl.BlockSpec((None, DEC_ROWS, BRANCH_WIDTH), lambda i, s, pt: (i, 0, 0)),
            scratch_shapes=[pltpu.VMEM((rows, BRANCH_WIDTH), F32),
                            pltpu.VMEM((rows, 1), F32)],
        ),
        compiler_params=_params("arbitrary", "arbitrary"),
        name="sb_attn_decode",
    )(page_table, bias_col, q, k_new, v_new, *([cache_k] * pages_per_step), *([cache_v] * pages_per_step))


def _sb_merge_kernel(x_ref, br_ref, tail_ref, mk_ref, mv_ref, gq_ref, wout_ref, y_ref):
    tail = tail_ref[...]
    bw = BRANCH_WIDTH
    y_ref[...] = _merge(x_ref[...], br_ref[...], tail[:, :bw], tail[:, bw:bw + MEM_WIDTH], tail[:, bw + MEM_WIDTH:],
                        mk_ref[...].astype(BF16), mv_ref[...].astype(BF16), gq_ref[...], wout_ref)


def _sb_merge_prompt(x, br, tail, mk, mv, g_q, w_out, *, tb):
    b, t, d = x.shape
    m = mk.shape[1]
    tail_w = tail.shape[-1]
    row_spec = lambda w: pl.BlockSpec((None, tb, w), lambda i, j: (i, j, 0))
    return pl.pallas_call(
        _sb_merge_kernel,
        out_shape=jax.ShapeDtypeStruct((b, t, d), F32),
        grid=(b, t // tb),
        in_specs=[
            row_spec(d), row_spec(BRANCH_WIDTH), row_spec(tail_w),
            pl.BlockSpec((None, m, MEM_WIDTH), lambda i, j: (i, 0, 0)),
            pl.BlockSpec((None, m, MEM_WIDTH), lambda i, j: (i, 0, 0)),
            _const_spec((1, HEAD_DIM)),
            _const_spec(w_out.shape),
        ],
        out_specs=row_spec(d),
        compiler_params=_params("arbitrary", "arbitrary"),
        name="sb_merge_prompt",
    )(x, br, tail, mk, mv, g_q.reshape(1, HEAD_DIM), w_out)


def kernel(x_prompt, x_sample, cache_mem_k, cache_mem_v, state_pool, cache_sb_k, cache_sb_v, page_table,
           mem_prompt, norm_g, w_in_pool, w_pool_grp, pool_scale, w_in_sb, sb_bias, mem_norm_g, w_mem_kv,
           mem_q_g, mem_k_g, w_out):
    b, t, d = x_prompt.shape
    bd, td, _ = x_sample.shape
    depth = norm_g.shape[0]
    n_pages = page_table.shape[1]
    page = cache_sb_k.shape[2]
    past = n_pages * page
    mem_len = mem_prompt.shape[1]
    tb = 256 if t % 256 == 0 else t
    tq = 256 if t % 256 == 0 else t
    rows = bd * DEC_ROWS

    w_in_pool_b = w_in_pool.astype(BF16)
    w_pool_grp_b = w_pool_grp.astype(BF16)
    w_in_sb_b = w_in_sb.astype(BF16)
    w_mem_kv_b = w_mem_kv.astype(BF16)
    w_out_b = w_out.astype(BF16)

    mk_all, mv_all = _mem_kv(mem_prompt, mem_norm_g, w_mem_kv_b, mem_k_g)
    cmk = cache_mem_k.reshape(depth, bd, mem_len, MEM_WIDTH)
    cmv = cache_mem_v.reshape(depth, bd, mem_len, MEM_WIDTH)

    def head_major_out(a, n):
        return a[:, :, :n].transpose(0, 2, 1, 3)

    xp = x_prompt
    xs = jnp.pad(x_sample, ((0, 0), (0, DEC_ROWS - td), (0, 0))).reshape(rows, d)
    pool_p, pool_s, sbk_p, sbv_p, sbk_s, sbv_s = [], [], [], [], [], []
    for i in range(depth):
        j = i // 2
        if i % 2 == 0:
            xp, st_p = _pool_layer_prompt(xp, norm_g[i], w_in_pool_b[j], w_pool_grp_b[j], pool_scale[j],
                                          mk_all[i], mv_all[i], mem_q_g[i], w_out_b[i], tb=tb)
            pool_p.append(st_p)
            state_tm = state_pool[j].transpose(1, 0, 2)
            br_s, u_tm, tail_s = _pool_layer_decode_proj(xs, norm_g[i], w_in_pool_b[j], w_pool_grp_b[j],
                                                         pool_scale[j], state_tm, n_tok=td, past=past)
            new_state_tm = jnp.concatenate([state_tm, u_tm[:td]], axis=0)[-POOL_STATE_LEN:]
            pool_s.append(new_state_tm.transpose(1, 0, 2))
        else:
            q, k, v, kb, vb, tail_p = _sb_proj(xp, norm_g[i], w_in_sb_b[j], tb=tb)
            sbk_p.append(head_major_out(k, t))
            sbv_p.append(head_major_out(v, t))
            br = _sb_attn_prompt(q, kb, vb, sb_bias[j], tq=tq)
            xp = _sb_merge_prompt(xp, br, tail_p, mk_all[i], mv_all[i], mem_q_g[i], w_out_b[i], tb=tb)

            qs, ks, vs, tail_s = _sb_proj_decode(xs, norm_g[i], w_in_sb_b[j])
            sbk_s.append(head_major_out(ks.reshape(SB_HEADS, bd, DEC_ROWS, HEAD_DIM).transpose(1, 0, 2, 3), td))
            sbv_s.append(head_major_out(vs.reshape(SB_HEADS, bd, DEC_ROWS, HEAD_DIM).transpose(1, 0, 2, 3), td))
            br_s = _sb_attn_decode(qs, ks, vs,
                                   cache_sb_k[j].transpose(0, 2, 1, 3), cache_sb_v[j].transpose(0, 2, 1, 3),
                                   page_table, sb_bias[j], pages_per_step=4).reshape(rows, BRANCH_WIDTH)
        mog = _mem_decode(tail_s.reshape(bd, DEC_ROWS, -1), cmk[i], cmv[i], mem_q_g[i])
        xs = _out_decode(xs, br_s, tail_s, mog.reshape(rows, MEM_WIDTH), w_out_b[i])

    ys = xs.reshape(bd, DEC_ROWS, d)[:, :td]
    mem_shape = (depth, b, mem_len, MEM_HEADS, HEAD_DIM)
    return (xp, ys, mk_all.reshape(mem_shape), mv_all.reshape(mem_shape),
            jnp.stack(pool_p), jnp.stack(pool_s),
            jnp.stack(sbk_p), jnp.stack(sbv_p), jnp.stack(sbk_s), jnp.stack(sbv_s))
```
